```python
import math
import jax, jax.numpy as jnp
from jax import lax
import numpy as np


D_MODEL = 1024
BATCH = 8
SEQ = 2048
DEPTH = 4
DEC_BATCH = 128
DEC_SEQ = 4
PAST_LEN = 2048
PAGE_SIZE = 128

N_MIXERS = 2
N_CONV_LAYERS = (DEPTH + 1) // 2
N_ATT_LAYERS = DEPTH // 2
CONV_W = 3
N_HEADS = 8
HEAD_DIM = D_MODEL // (2 * N_HEADS)
VAL_DIM = 2 * HEAD_DIM
ROT_DIM = HEAD_DIM // 4
ROPE_THETA = 500000.0
FFN_DIM = ((8 * D_MODEL // 3 + 255) // 256) * 256
Q_BLOCK = 128
NORM_EPS = 1e-6
SUBLN_EPS = 1e-5

kernel_name = 'hybrid_shortconv_diffattn_macaron_step'


def _rmsnorm(x, g, eps):
    xf = x.astype(jnp.float32)
    y = xf * lax.rsqrt(jnp.mean(xf * xf, axis=-1, keepdims=True) + eps)
    return (y * g.astype(jnp.float32)).astype(x.dtype)


def _swiglu(h, w_in, w_out):
    gate, up = jnp.split(h @ w_in, 2, axis=-1)
    return (jax.nn.silu(gate) * up) @ w_out


def _conv_mixer(h, w_in, w_conv, w_out, prev):
    s = h.shape[1]
    b_gate, c_gate, xin = jnp.split(h @ w_in, 3, axis=-1)
    u = c_gate * xin
    up = jnp.concatenate([prev, u], axis=1)
    v = w_conv[0] * up[:, 0:s]
    for j in range(1, CONV_W):
        v = v + w_conv[j] * up[:, j:j + s]
    y = (b_gate * v) @ w_out
    return y, up[:, -(CONV_W - 1):]


def _rope(x, pos):
    half = ROT_DIM // 2
    inv = jnp.power(ROPE_THETA, -jnp.arange(half, dtype=jnp.float32) * (2.0 / ROT_DIM))
    ang = pos.astype(jnp.float32)[:, None] * inv[None, :]
    cos = jnp.cos(ang)[None, :, None, None, :]
    sin = jnp.sin(ang)[None, :, None, None, :]
    xf = x.astype(jnp.float32)
    x1 = xf[..., :half]
    x2 = xf[..., half:ROT_DIM]
    out = jnp.concatenate([x1 * cos - x2 * sin, x2 * cos + x1 * sin, xf[..., ROT_DIM:]], axis=-1)
    return out.astype(x.dtype)


def _diff_qkv(h, w_qkv, pos):
    b, s, _ = h.shape
    q, k, v = jnp.split(h @ w_qkv, 3, axis=-1)
    q = _rope(q.reshape(b, s, N_HEADS, 2, HEAD_DIM), pos)
    k = _rope(k.reshape(b, s, N_HEADS, 2, HEAD_DIM), pos)
    v = v.reshape(b, s, N_HEADS, VAL_DIM)
    return q, k, v


def _diff_lambda(lq1, lk1, lq2, lk2, lam_init):
    f = jnp.float32
    return (jnp.exp(jnp.sum(lq1.astype(f) * lk1.astype(f)))
            - jnp.exp(jnp.sum(lq2.astype(f) * lk2.astype(f))) + lam_init)


def _diff_attn_prompt(q, k, v, lam):
    b, s = q.shape[0], q.shape[1]
    nqb = s // Q_BLOCK
    scale = HEAD_DIM ** -0.5
    qb = q.reshape(b, nqb, Q_BLOCK, N_HEADS, 2, HEAD_DIM).transpose(1, 0, 2, 3, 4, 5)
    kpos = jnp.arange(s)

    def block(args):
        qi, bi = args
        sc = jnp.einsum('bqhcd,bkhcd->bhcqk', qi, k).astype(jnp.float32) * scale
        qpos = bi * Q_BLOCK + jnp.arange(Q_BLOCK)
        mask = kpos[None, :] <= qpos[:, None]
        p = jax.nn.softmax(jnp.where(mask, sc, -jnp.inf), axis=-1)
        pd = (p[:, :, 0] - lam * p[:, :, 1]).astype(v.dtype)
        return jnp.einsum('bhqk,bkhe->bqhe', pd, v)

    o = lax.map(block, (qb, jnp.arange(nqb)))
    return o.transpose(1, 0, 2, 3, 4).reshape(b, s, N_HEADS, VAL_DIM)


def _diff_attn_sample(q, k_new, v_new, k_past, v_past, lam):
    t = q.shape[1]
    p_len = k_past.shape[1]
    scale = HEAD_DIM ** -0.5
    s_past = jnp.einsum('bqhcd,bkhcd->bhcqk', q, k_past).astype(jnp.float32) * scale
    s_new = jnp.einsum('bqhcd,bkhcd->bhcqk', q, k_new).astype(jnp.float32) * scale
    causal = jnp.arange(t)[None, :] <= jnp.arange(t)[:, None]
    s_new = jnp.where(causal, s_new, -jnp.inf)
    p = jax.nn.softmax(jnp.concatenate([s_past, s_new], axis=-1), axis=-1)
    pd = (p[:, :, 0] - lam * p[:, :, 1]).astype(v_new.dtype)
    return (jnp.einsum('bhqk,bkhe->bqhe', pd[..., :p_len], v_past)
            + jnp.einsum('bhqk,bkhe->bqhe', pd[..., p_len:], v_new))


def _diff_out(o, g_subln, lam_init, w_o):
    b, s = o.shape[0], o.shape[1]
    o = _rmsnorm(o, g_subln, SUBLN_EPS) * (1.0 - lam_init)
    return o.reshape(b, s, N_HEADS * VAL_DIM) @ w_o


def _run_group(x, pos, conv_prev, kv_past, p):
    b = x.shape[0]
    new_k, new_v, new_conv = [], [], []
    for i in range(DEPTH):
        x = x + 0.5 * _swiglu(_rmsnorm(x, p['g_ffn1'][i], NORM_EPS), p['w_ffn1_in'][i], p['w_ffn1_out'][i])
        h = _rmsnorm(x, p['g_mix'][i], NORM_EPS)
        j = i // N_MIXERS
        if i % N_MIXERS == 0:
            prev = jnp.zeros((b, CONV_W - 1, D_MODEL), x.dtype) if conv_prev is None else conv_prev[:, j]
            mix, st = _conv_mixer(h, p['w_conv_in'][j], p['w_conv'][j], p['w_conv_out'][j], prev)
            new_conv.append(st)
        else:
            lam_init = 0.8 - 0.6 * math.exp(-0.3 * i)
            q, k, v = _diff_qkv(h, p['w_qkv'][j], pos)
            lam = _diff_lambda(p['lambda_q1'][j], p['lambda_k1'][j], p['lambda_q2'][j], p['lambda_k2'][j], lam_init)
            if kv_past is None:
                o = _diff_attn_prompt(q, k, v, lam)
            else:
                cache_k, cache_v, page_table = kv_past
                k_past = cache_k[page_table, j].reshape(b, -1, N_HEADS, 2, HEAD_DIM)
                v_past = cache_v[page_table, j].reshape(b, -1, N_HEADS, VAL_DIM)
                o = _diff_attn_sample(q, k, v, k_past, v_past, lam)
            mix = _diff_out(o, p['g_subln'][j], lam_init, p['w_o'][j])
            new_k.append(k)
            new_v.append(v)
        x = x + mix
        x = x + 0.5 * _swiglu(_rmsnorm(x, p['g_ffn2'][i], NORM_EPS), p['w_ffn2_in'][i], p['w_ffn2_out'][i])
    y = _rmsnorm(x, p['g_final'], NORM_EPS)
    return y, jnp.stack(new_k, axis=1), jnp.stack(new_v, axis=1), jnp.stack(new_conv, axis=1)


def setup_inputs(seed: int = 0) -> dict:
    key = jax.random.key(seed)
    ks = jax.random.split(key, 32)
    f = jnp.float32
    n_pages = PAST_LEN // PAGE_SIZE
    n_used = DEC_BATCH * n_pages
    n_pool = (5 * n_used + 3) // 4

    def w(k, shape, fan_in):
        return jax.random.normal(k, shape, f) * (fan_in ** -0.5)

    def gain(k, shape):
        return 1.0 + 0.02 * jax.random.normal(k, shape, f)

    page_table = jax.random.permutation(ks[5], n_pool)[:n_used].reshape(DEC_BATCH, n_pages).astype(jnp.int32)
    return {
        'x_prompt': jax.random.normal(ks[0], (BATCH, SEQ, D_MODEL), f),
        'x_sample': jax.random.normal(ks[1], (DEC_BATCH, DEC_SEQ, D_MODEL), f),
        'cache_k': jax.random.normal(ks[2], (n_pool, N_ATT_LAYERS, PAGE_SIZE, N_HEADS, 2, HEAD_DIM), f),
        'cache_v': jax.random.normal(ks[3], (n_pool, N_ATT_LAYERS, PAGE_SIZE, N_HEADS, VAL_DIM), f),
        'state_conv': jax.random.normal(ks[4], (DEC_BATCH, N_CONV_LAYERS, CONV_W - 1, D_MODEL), f),
        'page_table': page_table,
        'g_ffn1': gain(ks[6], (DEPTH, D_MODEL)),
        'w_ffn1_in': w(ks[7], (DEPTH, D_MODEL, 2 * FFN_DIM), D_MODEL),
        'w_ffn1_out': w(ks[8], (DEPTH, FFN_DIM, D_MODEL), FFN_DIM),
        'g_mix': gain(ks[9], (DEPTH, D_MODEL)),
        'w_conv_in': w(ks[10], (N_CONV_LAYERS, D_MODEL, 3 * D_MODEL), D_MODEL),
        'w_conv': w(ks[11], (N_CONV_LAYERS, CONV_W, D_MODEL), CONV_W),
        'w_conv_out': w(ks[12], (N_CONV_LAYERS, D_MODEL, D_MODEL), D_MODEL),
        'w_qkv': w(ks[13], (N_ATT_LAYERS, D_MODEL, 3 * N_HEADS * VAL_DIM), D_MODEL),
        'lambda_q1': 0.1 * jax.random.normal(ks[14], (N_ATT_LAYERS, HEAD_DIM), f),
        'lambda_k1': 0.1 * jax.random.normal(ks[15], (N_ATT_LAYERS, HEAD_DIM), f),
        'lambda_q2': 0.1 * jax.random.normal(ks[16], (N_ATT_LAYERS, HEAD_DIM), f),
        'lambda_k2': 0.1 * jax.random.normal(ks[17], (N_ATT_LAYERS, HEAD_DIM), f),
        'g_subln': gain(ks[18], (N_ATT_LAYERS, VAL_DIM)),
        'w_o': w(ks[19], (N_ATT_LAYERS, N_HEADS * VAL_DIM, D_MODEL), N_HEADS * VAL_DIM),
        'g_ffn2': gain(ks[20], (DEPTH, D_MODEL)),
        'w_ffn2_in': w(ks[21], (DEPTH, D_MODEL, 2 * FFN_DIM), D_MODEL),
        'w_ffn2_out': w(ks[22], (DEPTH, FFN_DIM, D_MODEL), FFN_DIM),
        'g_final': gain(ks[23], (D_MODEL,)),
    }


def reference(x_prompt, x_sample, cache_k, cache_v, state_conv, page_table,
              g_ffn1, w_ffn1_in, w_ffn1_out, g_mix, w_conv_in, w_conv, w_conv_out,
              w_qkv, lambda_q1, lambda_k1, lambda_q2, lambda_k2, g_subln, w_o,
              g_ffn2, w_ffn2_in, w_ffn2_out, g_final):
    params = {
        'g_ffn1': g_ffn1, 'w_ffn1_in': w_ffn1_in, 'w_ffn1_out': w_ffn1_out,
        'g_mix': g_mix, 'w_conv_in': w_conv_in, 'w_conv': w_conv, 'w_conv_out': w_conv_out,
        'w_qkv': w_qkv, 'lambda_q1': lambda_q1, 'lambda_k1': lambda_k1,
        'lambda_q2': lambda_q2, 'lambda_k2': lambda_k2, 'g_subln': g_subln, 'w_o': w_o,
        'g_ffn2': g_ffn2, 'w_ffn2_in': w_ffn2_in, 'w_ffn2_out': w_ffn2_out, 'g_final': g_final,
    }
    pos_prompt = jnp.arange(x_prompt.shape[1])
    past_len = page_table.shape[1] * cache_k.shape[2]
    pos_sample = past_len + jnp.arange(x_sample.shape[1])
    y_prompt, k_rows_prompt, v_rows_prompt, conv_prompt = _run_group(x_prompt, pos_prompt, None, None, params)
    y_sample, k_rows_sample, v_rows_sample, conv_sample = _run_group(
        x_sample, pos_sample, state_conv, (cache_k, cache_v, page_table), params)
    return (y_prompt, y_sample, k_rows_prompt, v_rows_prompt, conv_prompt, k_rows_sample, v_rows_sample, conv_sample)
```

```python
import functools
import math

import jax
import jax.numpy as jnp
from jax import lax
from jax.experimental import pallas as pl
from jax.experimental.pallas import tpu as pltpu

D_MODEL = 1024
FFN_DIM = 2816
N_HEADS = 8
HEAD_DIM = 64
VAL_DIM = 128
ROT_DIM = 16
ROPE_THETA = 500000.0
CONV_W = 3
NORM_EPS = 1e-6
SUBLN_EPS = 1e-5
N_MIXERS = 2

V7X_LANES = 128
V7X_SUBLANES = 8
V7X_VMEM_LIMIT_BYTES = 58 * 1024 * 1024

TOKEN_TILE = 512
FFN_CHUNKS = ((0, 768), (768, 768), (1536, 768), (2304, 512))
ATTN_Q_TILE = 512
ATTN_K_TILE = 512

F32 = jnp.float32
BF16 = jnp.bfloat16
NEG_INF = float("-inf")


def _params(*semantics):
    return pltpu.CompilerParams(
        dimension_semantics=semantics, vmem_limit_bytes=V7X_VMEM_LIMIT_BYTES)


def _rms(x, g, eps):
    return x * lax.rsqrt(jnp.mean(x * x, axis=-1, keepdims=True) + eps) * g


def _resident(shape, index):
    return pl.BlockSpec(shape, lambda *_: index, pipeline_mode=pl.Buffered(1))


def _ffn_kernel(x_ref, g_ref, win_ref, wout_ref, o_ref, act_ref):
    x = x_ref[...]
    h = _rms(x, g_ref[...], NORM_EPS).astype(BF16)
    for c0, cw in FFN_CHUNKS:
        gate = jnp.dot(h, win_ref[:, c0:c0 + cw], preferred_element_type=F32)
        up = jnp.dot(h, win_ref[:, FFN_DIM + c0:FFN_DIM + c0 + cw], preferred_element_type=F32)
        act_ref[:, c0:c0 + cw] = (gate / (1.0 + jnp.exp(-gate)) * up).astype(BF16)
    y = jnp.dot(act_ref[...], wout_ref[...], preferred_element_type=F32)
    o_ref[...] = x + 0.5 * y


def _ffn(x, g, w_in, w_out, layer):
    n_tok = x.shape[0]
    tile = pl.BlockSpec((TOKEN_TILE, D_MODEL), lambda i: (i, 0))
    return pl.pallas_call(
        _ffn_kernel,
        grid=(n_tok // TOKEN_TILE,),
        in_specs=[
            tile,
            _resident((None, 1, D_MODEL), (layer, 0, 0)),
            _resident((None, D_MODEL, 2 * FFN_DIM), (layer, 0, 0)),
            _resident((None, FFN_DIM, D_MODEL), (layer, 0, 0)),
        ],
        out_specs=tile,
        out_shape=jax.ShapeDtypeStruct(x.shape, F32),
        scratch_shapes=[pltpu.VMEM((TOKEN_TILE, FFN_DIM), BF16)],
        compiler_params=_params("arbitrary"),
        name="ffn",
    )(x, g, w_in, w_out)


def _conv_gate(x_ref, g_ref, win_ref):
    x = x_ref[...]
    h = _rms(x, g_ref[...], NORM_EPS).astype(BF16)
    bcx = jnp.dot(h, win_ref[...], preferred_element_type=F32)
    b_gate = bcx[:, :D_MODEL]
    u = bcx[:, D_MODEL:2 * D_MODEL] * bcx[:, 2 * D_MODEL:]
    return x, b_gate, u


def _conv_finish(x, b_gate, u, u1, u2, wc_ref, wout_ref, o_ref):
    v = wc_ref[0:1, :] * u2 + wc_ref[1:2, :] * u1 + wc_ref[2:3, :] * u
    y = jnp.dot((b_gate * v).astype(BF16), wout_ref[...], preferred_element_type=F32)
    o_ref[...] = x + y


def _conv_prompt_kernel(tiles_per_seq, x_ref, g_ref, win_ref, wc_ref, wout_ref,
                        o_ref, tail_ref, ubuf):
    @pl.when(pl.program_id(0) % tiles_per_seq == 0)
    def _():
        ubuf[0:V7X_SUBLANES, :] = jnp.zeros((V7X_SUBLANES, D_MODEL), F32)

    x, b_gate, u = _conv_gate(x_ref, g_ref, win_ref)
    ubuf[V7X_SUBLANES:, :] = u
    u1 = ubuf[V7X_SUBLANES - 1:V7X_SUBLANES - 1 + TOKEN_TILE, :]
    u2 = ubuf[V7X_SUBLANES - 2:V7X_SUBLANES - 2 + TOKEN_TILE, :]
    _conv_finish(x, b_gate, u, u1, u2, wc_ref, wout_ref, o_ref)
    tail = u[TOKEN_TILE - V7X_SUBLANES:, :]
    tail_ref[...] = tail
    ubuf[0:V7X_SUBLANES, :] = tail


def _conv_prompt(x, g, w_in, w_conv, w_out, layer, seq_len):
    n_tiles = x.shape[0] // TOKEN_TILE
    tile = pl.BlockSpec((TOKEN_TILE, D_MODEL), lambda i: (i, 0))
    return pl.pallas_call(
        functools.partial(_conv_prompt_kernel, seq_len // TOKEN_TILE),
        grid=(n_tiles,),
        in_specs=[
            tile,
            _resident((None, 1, D_MODEL), (2 * layer, 0, 0)),
            _resident((None, D_MODEL, 3 * D_MODEL), (layer, 0, 0)),
            _resident((None, CONV_W, D_MODEL), (layer, 0, 0)),
            _resident((None, D_MODEL, D_MODEL), (layer, 0, 0)),
        ],
        out_specs=[tile, pl.BlockSpec((None, V7X_SUBLANES, D_MODEL), lambda i: (i, 0, 0))],
        out_shape=[jax.ShapeDtypeStruct(x.shape, F32),
                   jax.ShapeDtypeStruct((n_tiles, V7X_SUBLANES, D_MODEL), F32)],
        scratch_shapes=[pltpu.VMEM((V7X_SUBLANES + TOKEN_TILE, D_MODEL), F32)],
        compiler_params=_params("arbitrary"),
        name="conv_prompt",
    )(x, g, w_in, w_conv, w_out)


def _conv_sample_kernel(n_seq, x_ref, g_ref, win_ref, wc_ref, wout_ref, p0_ref, p1_ref,
                        o_ref, st_ref):
    x, b_gate, u = _conv_gate(x_ref, g_ref, win_ref)
    n = x.shape[0]
    u1 = jnp.concatenate([p1_ref[...], u[:n - n_seq, :]], axis=0)
    u2 = jnp.concatenate([p0_ref[...], p1_ref[...], u[:n - 2 * n_seq, :]], axis=0)
    _conv_finish(x, b_gate, u, u1, u2, wc_ref, wout_ref, o_ref)
    st_ref[...] = u[n - 2 * n_seq:, :]


def _conv_sample(x, g, w_in, w_conv, w_out, prev0, prev1, layer):
    n_tok = x.shape[0]
    n_seq = prev0.shape[0]
    full = pl.BlockSpec((n_tok, D_MODEL), lambda i: (0, 0))
    prev = pl.BlockSpec((n_seq, D_MODEL), lambda i: (0, 0))
    return pl.pallas_call(
        functools.partial(_conv_sample_kernel, n_seq),
        grid=(1,),
        in_specs=[
            full,
            _resident((None, 1, D_MODEL), (2 * layer, 0, 0)),
            _resident((None, D_MODEL, 3 * D_MODEL), (layer, 0, 0)),
            _resident((None, CONV_W, D_MODEL), (layer, 0, 0)),
            _resident((None, D_MODEL, D_MODEL), (layer, 0, 0)),
            prev, prev,
        ],
        out_specs=[full, pl.BlockSpec((2 * n_seq, D_MODEL), lambda i: (0, 0))],
        out_shape=[jax.ShapeDtypeStruct(x.shape, F32),
                   jax.ShapeDtypeStruct((2 * n_seq, D_MODEL), F32)],
        compiler_params=_params("arbitrary"),
        name="conv_sample",
    )(x, g, w_in, w_conv, w_out, prev0, prev1)


def _rope_tables(pos):
    half = ROT_DIM // 2
    inv = jnp.power(ROPE_THETA, -jnp.arange(half, dtype=F32) * (2.0 / ROT_DIM))
    ang = pos.astype(F32)[:, None] * inv[None, :]
    cos, sin = jnp.cos(ang), jnp.sin(ang)
    n = pos.shape[0]
    ones = jnp.ones((n, HEAD_DIM - ROT_DIM), F32)
    zeros = jnp.zeros((n, HEAD_DIM - ROT_DIM), F32)
    zhalf = jnp.zeros((n, half), F32)
    c = jnp.concatenate([cos, cos, ones], axis=1)
    s1 = jnp.concatenate([-sin, zhalf, zeros], axis=1)
    s2 = jnp.concatenate([zhalf, sin, zeros], axis=1)
    rep = V7X_LANES // HEAD_DIM
    return tuple(jnp.tile(t, (1, rep)) for t in (c, s1, s2))


def _qkv_kernel(x_ref, g_ref, w_ref, c_ref, s1_ref, s2_ref,
                qb_ref, kf_ref, vf_ref, kb_ref, vb_ref):
    h = _rms(x_ref[...], g_ref[...], NORM_EPS).astype(BF16)
    qkv = jnp.dot(h, w_ref[...], preferred_element_type=F32)
    c, s1, s2 = c_ref[...], s1_ref[...], s2_ref[...]
    half = ROT_DIM // 2
    hv = N_HEADS * VAL_DIM

    def rope(z):
        return (z * c + pltpu.roll(z, V7X_LANES - half, 1) * s1
                + pltpu.roll(z, half, 1) * s2)

    for blk in range(hv // V7X_LANES):
        cols = slice(blk * V7X_LANES, (blk + 1) * V7X_LANES)
        q = rope(qkv[:, cols])
        qb_ref[:, cols] = (q * (HEAD_DIM ** -0.5)).astype(BF16)
        k = rope(qkv[:, hv + blk * V7X_LANES:hv + (blk + 1) * V7X_LANES])
        kf_ref[:, cols] = k
        kb_ref[:, cols] = k.astype(BF16)
    v = qkv[:, 2 * hv:]
    vf_ref[...] = v
    vb_ref[...] = v.astype(BF16)


def _qkv(x, g, w_qkv, tables, layer, table_tiles):
    n_tok = x.shape[0]
    hv = N_HEADS * VAL_DIM
    tile = pl.BlockSpec((TOKEN_TILE, hv), lambda i: (i, 0))
    tab = pl.BlockSpec((TOKEN_TILE, V7X_LANES), lambda i: (i % table_tiles, 0))
    shp = lambda dt: jax.ShapeDtypeStruct((n_tok, hv), dt)
    return pl.pallas_call(
        _qkv_kernel,
        grid=(n_tok // TOKEN_TILE,),
        in_specs=[
            tile,
            _resident((None, 1, D_MODEL), (2 * layer + 1, 0, 0)),
            _resident((None, D_MODEL, 3 * hv), (layer, 0, 0)),
            tab, tab, tab,
        ],
        out_specs=[tile] * 5,
        out_shape=[shp(BF16), shp(F32), shp(F32), shp(BF16), shp(BF16)],
        compiler_params=_params("arbitrary"),
        name="qkv_rope",
    )(x, g, w_qkv, *tables)


def _diff_lambda(lam_ref, lam_init):
    lv = lam_ref[...]
    s1 = jnp.sum(lv[0:1, :] * lv[1:2, :], axis=-1, keepdims=True)
    s2 = jnp.sum(lv[2:3, :] * lv[3:4, :], axis=-1, keepdims=True)
    return jnp.exp(s1) - jnp.exp(s2) + lam_init


def _map_mask(q, c):
    lane = lax.broadcasted_iota(jnp.int32, q.shape, 1)
    keep = (lane < HEAD_DIM) if c == 0 else (lane >= HEAD_DIM)
    return jnp.where(keep, q, jnp.zeros_like(q))


_NT = (((1,), (1,)), ((), ()))


def _attn_prompt_kernel(lam_init, lam_ref, q_ref, k_ref, v_ref, o_ref):
    qi = pl.program_id(2)
    tq, tk = ATTN_Q_TILE, ATTN_K_TILE
    q = q_ref[...]

    def step(qc, kj, vj, carry, mask):
        m, l, acc = carry
        s = lax.dot_general(qc, kj, _NT, preferred_element_type=F32)
        if mask is not None:
            s = jnp.where(mask, s, NEG_INF)
        m_new = jnp.maximum(m, jnp.max(s, axis=-1, keepdims=True))
        alpha = jnp.exp(m - m_new)
        p = jnp.exp(s - m_new)
        l = alpha * l + jnp.sum(p, axis=-1, keepdims=True)
        acc = alpha * acc + jnp.dot(p.astype(BF16), vj, preferred_element_type=F32)
        return m_new, l, acc

    row = lax.broadcasted_iota(jnp.int32, (tq, tk), 0)
    col = lax.broadcasted_iota(jnp.int32, (tq, tk), 1)
    causal = col <= row
    outs = []
    for c in range(2):
        qc = _map_mask(q, c)

        def body(j, carry, qc=qc):
            start = pl.multiple_of(j * tk, tk)
            return step(qc, k_ref[pl.ds(start, tk), :], v_ref[pl.ds(start, tk), :], carry, None)

        init = (jnp.full((tq, 1), NEG_INF, F32), jnp.zeros((tq, 1), F32),
                jnp.zeros((tq, VAL_DIM), F32))
        carry = lax.fori_loop(0, qi, body, init)
        start = pl.multiple_of(qi * tk, tk)
        _, l, acc = step(qc, k_ref[pl.ds(start, tk), :], v_ref[pl.ds(start, tk), :], carry, causal)
        outs.append(acc / l)
    o_ref[...] = outs[0] - _diff_lambda(lam_ref, lam_init) * outs[1]


def _attn_prompt(lams, qb, kb, vb, batch, seq_len, lam_init):
    nq = seq_len // ATTN_Q_TILE
    q_spec = pl.BlockSpec((ATTN_Q_TILE, VAL_DIM), lambda b, h, i: (b * nq + i, h))
    kv_spec = pl.BlockSpec((seq_len, VAL_DIM), lambda b, h, i: (b, h))
    return pl.pallas_call(
        functools.partial(_attn_prompt_kernel, lam_init),
        grid=(batch, N_HEADS, nq),
        in_specs=[pl.BlockSpec(lams.shape, lambda b, h, i: (0, 0)), q_spec, kv_spec, kv_spec],
        out_specs=q_spec,
        out_shape=jax.ShapeDtypeStruct(qb.shape, F32),
        compiler_params=_params("arbitrary", "arbitrary", "arbitrary"),
        name="attn_prompt",
    )(lams, qb, kb, vb)


def _attn_sample_kernel(layer, n_pages, page_size, n_new, lam_init,
                        pt_ref, lam_ref, q_ref, k_ref, v_ref, ck_ref, cv_ref, o_ref,
                        kbuf, vbuf, sem_k, sem_v):
    b = pl.program_id(0)
    nb = pl.num_programs(0)
    rows_per_page = page_size * N_HEADS

    def copies(bb, slot):
        out = []
        for p in range(n_pages):
            page = pt_ref[bb, p]
            out.append(pltpu.make_async_copy(
                ck_ref.at[page, layer],
                kbuf.at[slot, :, pl.ds(p * page_size, page_size)], sem_k.at[slot]))
            out.append(pltpu.make_async_copy(
                cv_ref.at[page, layer],
                vbuf.at[slot, pl.ds(p * rows_per_page, rows_per_page), :], sem_v.at[slot]))
        return out

    slot = b % 2

    @pl.when(b == 0)
    def _():
        for cp in copies(0, 0):
            cp.start()

    @pl.when(b + 1 < nb)
    def _():
        for cp in copies(b + 1, 1 - slot):
            cp.start()

    for cp in copies(b, slot):
        cp.wait()

    lam = _diff_lambda(lam_ref, lam_init)
    q = q_ref[...]
    k_new = k_ref[...]
    v_new = v_ref[...]
    m8 = q.shape[0]
    past = n_pages * page_size
    row = lax.broadcasted_iota(jnp.int32, (m8, m8), 0)
    col = lax.broadcasted_iota(jnp.int32, (m8, m8), 1)
    new_ok = (col <= row) & (col < n_new)
    for h in range(N_HEADS):
        probs = []
        for c in range(2):
            lo = (2 * h + c) * HEAD_DIM
            qhc = q[:, lo:lo + HEAD_DIM]
            kt = kbuf[slot, lo:lo + HEAD_DIM, :].astype(BF16)
            s_past = jnp.dot(qhc, kt, preferred_element_type=F32)
            s_new = lax.dot_general(qhc, k_new[:, lo:lo + HEAD_DIM], _NT,
                                    preferred_element_type=F32)
            s_new = jnp.where(new_ok, s_new, NEG_INF)
            m = jnp.maximum(jnp.max(s_past, axis=-1, keepdims=True),
                            jnp.max(s_new, axis=-1, keepdims=True))
            p_past = jnp.exp(s_past - m)
            p_new = jnp.exp(s_new - m)
            inv = 1.0 / (jnp.sum(p_past, axis=-1, keepdims=True)
                         + jnp.sum(p_new, axis=-1, keepdims=True))
            probs.append((p_past * inv, p_new * inv))
        pd_past = (probs[0][0] - lam * probs[1][0]).astype(BF16)
        pd_new = (probs[0][1] - lam * probs[1][1]).astype(BF16)
        vh = vbuf[slot, pl.ds(h, past, stride=N_HEADS), :].astype(BF16)
        o = (jnp.dot(pd_past, vh, preferred_element_type=F32)
             + jnp.dot(pd_new, v_new[:, h * VAL_DIM:(h + 1) * VAL_DIM],
                       preferred_element_type=F32))
        o_ref[:, h * VAL_DIM:(h + 1) * VAL_DIM] = o


def _attn_sample(page_table, lams, q8, k8, v8, ck, cv, layer, lam_init, n_new):
    n_seq, n_pages = page_table.shape
    page_size = ck.shape[-1]
    past = n_pages * page_size
    hv = N_HEADS * VAL_DIM
    m8 = q8.shape[1]
    per_seq = pl.BlockSpec((None, m8, hv), lambda b, pt: (b, 0, 0))
    grid_spec = pltpu.PrefetchScalarGridSpec(
        num_scalar_prefetch=1,
        grid=(n_seq,),
        in_specs=[
            pl.BlockSpec(lams.shape, lambda b, pt: (0, 0)),
            per_seq, per_seq, per_seq,
            pl.BlockSpec(memory_space=pl.ANY),
            pl.BlockSpec(memory_space=pl.ANY),
        ],
        out_specs=per_seq,
        scratch_shapes=[
            pltpu.VMEM((2, 2 * N_HEADS * HEAD_DIM, past), F32),
            pltpu.VMEM((2, past * N_HEADS, VAL_DIM), F32),
            pltpu.SemaphoreType.DMA((2,)),
            pltpu.SemaphoreType.DMA((2,)),
        ],
    )
    return pl.pallas_call(
        functools.partial(_attn_sample_kernel, layer, n_pages, page_size, n_new, lam_init),
        grid_spec=grid_spec,
        out_shape=jax.ShapeDtypeStruct((n_seq, m8, hv), F32),
        compiler_params=_params("arbitrary"),
        name="attn_sample",
    )(page_table, lams, q8, k8, v8, ck, cv)


def _attn_out_kernel(lam_init, x_ref, o_ref, g_ref, wo_ref, out_ref, hb_ref):
    for h in range(N_HEADS):
        cols = slice(h * VAL_DIM, (h + 1) * VAL_DIM)
        y = _rms(o_ref[:, cols], g_ref[...], SUBLN_EPS) * (1.0 - lam_init)
        hb_ref[:, cols] = y.astype(BF16)
    out_ref[...] = x_ref[...] + jnp.dot(hb_ref[...], wo_ref[...], preferred_element_type=F32)


def _attn_out(x, o, g_subln, w_o, layer, lam_init):
    n_tok = x.shape[0]
    tile = pl.BlockSpec((TOKEN_TILE, D_MODEL), lambda i: (i, 0))
    return pl.pallas_call(
        functools.partial(_attn_out_kernel, lam_init),
        grid=(n_tok // TOKEN_TILE,),
        in_specs=[
            tile, tile,
            _resident((None, 1, VAL_DIM), (layer, 0, 0)),
            _resident((None, N_HEADS * VAL_DIM, D_MODEL), (layer, 0, 0)),
        ],
        out_specs=tile,
        out_shape=jax.ShapeDtypeStruct(x.shape, F32),
        scratch_shapes=[pltpu.VMEM((TOKEN_TILE, N_HEADS * VAL_DIM), BF16)],
        compiler_params=_params("arbitrary"),
        name="attn_out",
    )(x, o, g_subln, w_o)


def _final_norm_kernel(x_ref, g_ref, o_ref):
    o_ref[...] = _rms(x_ref[...], g_ref[...], NORM_EPS)


def _final_norm(x, g):
    tile = pl.BlockSpec((TOKEN_TILE, D_MODEL), lambda i: (i, 0))
    return pl.pallas_call(
        _final_norm_kernel,
        grid=(x.shape[0] // TOKEN_TILE,),
        in_specs=[tile, pl.BlockSpec((1, D_MODEL), lambda i: (0, 0))],
        out_specs=tile,
        out_shape=jax.ShapeDtypeStruct(x.shape, F32),
        compiler_params=_params("arbitrary"),
        name="final_norm",
    )(x, g)


def _pad_rows(x, rows):
    return jnp.concatenate(
        [x, jnp.zeros((x.shape[0], rows - x.shape[1], x.shape[2]), x.dtype)], axis=1)


def kernel(x_prompt, x_sample, cache_k, cache_v, state_conv, page_table, g_ffn1, w_ffn1_in, w_ffn1_out, g_mix, w_conv_in, w_conv, w_conv_out, w_qkv, lambda_q1, lambda_k1, lambda_q2, lambda_k2, g_subln, w_o, g_ffn2, w_ffn2_in, w_ffn2_out, g_final):
    batch, seq_len, _ = x_prompt.shape
    n_seq, n_new, _ = x_sample.shape
    depth = g_ffn1.shape[0]
    n_pool, n_att, page_size = cache_k.shape[:3]
    past_len = page_table.shape[1] * page_size
    hv = N_HEADS * VAL_DIM

    w1i, w1o = w_ffn1_in.astype(BF16), w_ffn1_out.astype(BF16)
    w2i, w2o = w_ffn2_in.astype(BF16), w_ffn2_out.astype(BF16)
    wci, wco = w_conv_in.astype(BF16), w_conv_out.astype(BF16)
    wqkv, wo = w_qkv.astype(BF16), w_o.astype(BF16)
    g1 = g_ffn1.reshape(depth, 1, D_MODEL)
    g2 = g_ffn2.reshape(depth, 1, D_MODEL)
    gm = g_mix.reshape(depth, 1, D_MODEL)
    gs = g_subln.reshape(n_att, 1, VAL_DIM)
    lams = jnp.stack([lambda_q1, lambda_k1, lambda_q2, lambda_k2], axis=1)

    ck = jnp.transpose(cache_k, (0, 1, 3, 4, 5, 2)).reshape(
        n_pool, n_att, 2 * N_HEADS * HEAD_DIM, page_size)
    cv = cache_v.reshape(n_pool, n_att, page_size * N_HEADS, VAL_DIM)

    tab_p = _rope_tables(jnp.arange(seq_len))
    tab_s = _rope_tables(past_len + jnp.arange(n_seq * n_new) // n_seq)

    xp = x_prompt.reshape(batch * seq_len, D_MODEL)
    xs = jnp.transpose(x_sample, (1, 0, 2)).reshape(n_new * n_seq, D_MODEL)

    kp, vp, cp, ks, vs, cs = [], [], [], [], [], []
    for i in range(depth):
        xp = _ffn(xp, g1, w1i, w1o, i)
        xs = _ffn(xs, g1, w1i, w1o, i)
        j = i // N_MIXERS
        if i % N_MIXERS == 0:
            xp, tails = _conv_prompt(xp, gm, wci, w_conv, wco, j, seq_len)
            tps = seq_len // TOKEN_TILE
            cp.append(tails.reshape(batch, tps, V7X_SUBLANES, D_MODEL)[:, -1, -(CONV_W - 1):, :])
            xs, st = _conv_sample(xs, gm, wci, w_conv, wco,
                                  state_conv[:, j, 0, :], state_conv[:, j, 1, :], j)
            cs.append(jnp.transpose(st.reshape(CONV_W - 1, n_seq, D_MODEL), (1, 0, 2)))
        else:
            lam_init = 0.8 - 0.6 * math.exp(-0.3 * i)
            qb, kf, vf, kb, vb = _qkv(xp, gm, wqkv, tab_p, j, seq_len // TOKEN_TILE)
            o = _attn_prompt(lams[j], qb, kb, vb, batch, seq_len, lam_init)
            xp = _attn_out(xp, o, gs, wo, j, lam_init)
            kp.append(kf.reshape(batch, seq_len, N_HEADS, 2, HEAD_DIM))
            vp.append(vf.reshape(batch, seq_len, N_HEADS, VAL_DIM))

            qb, kf, vf, kb, vb = _qkv(xs, gm, wqkv, tab_s, j, 1)
            to_seq = lambda a: _pad_rows(
                jnp.transpose(a.reshape(n_new, n_seq, hv), (1, 0, 2)), V7X_SUBLANES)
            o8 = _attn_sample(page_table, lams[j], to_seq(qb), to_seq(kb), to_seq(vb),
                              ck, cv, j, lam_init, n_new)
            o = jnp.transpose(o8[:, :n_new, :], (1, 0, 2)).reshape(n_new * n_seq, hv)
            xs = _attn_out(xs, o, gs, wo, j, lam_init)
            ks.append(jnp.transpose(kf.reshape(n_new, n_seq, N_HEADS, 2, HEAD_DIM), (1, 0, 2, 3, 4)))
            vs.append(jnp.transpose(vf.reshape(n_new, n_seq, N_HEADS, VAL_DIM), (1, 0, 2, 3)))
        xp = _ffn(xp, g2, w2i, w2o, i)
        xs = _ffn(xs, g2, w2i, w2o, i)

    gf = g_final.reshape(1, D_MODEL)
    y_prompt = _final_norm(xp, gf).reshape(batch, seq_len, D_MODEL)
    y_sample = jnp.transpose(
        _final_norm(xs, gf).reshape(n_new, n_seq, D_MODEL), (1, 0, 2))
    return (y_prompt, y_sample,
            jnp.stack(kp, axis=1), jnp.stack(vp, axis=1), jnp.stack(cp, axis=1),
            jnp.stack(ks, axis=1), jnp.stack(vs, axis=1), jnp.stack(cs, axis=1))
```

```python
import functools
import math

import jax
import jax.numpy as jnp
from jax import lax
from jax.experimental import pallas as pl
from jax.experimental.pallas import tpu as pltpu

D_MODEL = 1024
FFN_DIM = 2816
N_HEADS = 8
HEAD_DIM = 64
VAL_DIM = 128
HV = N_HEADS * VAL_DIM
ROT_DIM = 16
ROPE_THETA = 500000.0
CONV_W = 3
NORM_EPS = 1e-6
SUBLN_EPS = 1e-5
N_MIXERS = 2

V7X_LANES = 128
V7X_SUBLANES = 8
V7X_VMEM_LIMIT_BYTES = 58 * 1024 * 1024

TOKEN_TILE = 512
FFN_CHUNKS = ((0, 768), (768, 768), (1536, 768), (2304, 512))
ATTN_Q_TILE = 256

F32 = jnp.float32
BF16 = jnp.bfloat16
NEG_INF = float("-inf")
_NT = (((1,), (1,)), ((), ()))


def _params(*semantics):
    return pltpu.CompilerParams(
        dimension_semantics=semantics, vmem_limit_bytes=V7X_VMEM_LIMIT_BYTES)


def _rms(x, g, eps):
    return x * lax.rsqrt(jnp.mean(x * x, axis=-1, keepdims=True) + eps) * g


def _resident(shape, index):
    return pl.BlockSpec(shape, lambda *_: index, pipeline_mode=pl.Buffered(1))


def _row_tile(offset=0):
    return pl.BlockSpec((TOKEN_TILE, D_MODEL), lambda i: (i + offset, 0))


_ANY = pl.BlockSpec(memory_space=pl.ANY)


def _ffn_body(x, g_ref, win_ref, wout_ref, act_ref):
    h = _rms(x, g_ref[...], NORM_EPS).astype(BF16)
    for c0, cw in FFN_CHUNKS:
        gate = jnp.dot(h, win_ref[:, c0:c0 + cw], preferred_element_type=F32)
        up = jnp.dot(h, win_ref[:, FFN_DIM + c0:FFN_DIM + c0 + cw], preferred_element_type=F32)
        act_ref[:, c0:c0 + cw] = (gate / (1.0 + jnp.exp(-gate)) * up).astype(BF16)
    return x + 0.5 * jnp.dot(act_ref[...], wout_ref[...], preferred_element_type=F32)


def _ffn_kernel(x_ref, g_ref, win_ref, wout_ref, o_ref, act_ref):
    o_ref[...] = _ffn_body(x_ref[...], g_ref, win_ref, wout_ref, act_ref)


def _ffn_first_kernel(n_prompt_tiles, xp_ref, xs_ref, g_ref, win_ref, wout_ref, o_ref, act_ref):
    x = jnp.where(pl.program_id(0) < n_prompt_tiles, xp_ref[...], xs_ref[...])
    o_ref[...] = _ffn_body(x, g_ref, win_ref, wout_ref, act_ref)


def _ffn_weight_specs(layer):
    return [
        _resident((None, 1, D_MODEL), (layer, 0, 0)),
        _resident((None, D_MODEL, 2 * FFN_DIM), (layer, 0, 0)),
        _resident((None, FFN_DIM, D_MODEL), (layer, 0, 0)),
    ]


def _ffn(x, g, w_in, w_out, layer):
    return pl.pallas_call(
        _ffn_kernel,
        grid=(x.shape[0] // TOKEN_TILE,),
        in_specs=[_row_tile()] + _ffn_weight_specs(layer),
        out_specs=_row_tile(),
        out_shape=jax.ShapeDtypeStruct(x.shape, F32),
        scratch_shapes=[pltpu.VMEM((TOKEN_TILE, FFN_DIM), BF16)],
        input_output_aliases={0: 0},
        compiler_params=_params("arbitrary"),
        name="ffn",
    )(x, g, w_in, w_out)


def _ffn_first(xp, xs, g, w_in, w_out, layer):
    n_p = xp.shape[0] // TOKEN_TILE
    n_s = xs.shape[0] // TOKEN_TILE
    return pl.pallas_call(
        functools.partial(_ffn_first_kernel, n_p),
        grid=(n_p + n_s,),
        in_specs=[
            pl.BlockSpec((TOKEN_TILE, D_MODEL), lambda i: (jnp.minimum(i, n_p - 1), 0)),
            pl.BlockSpec((TOKEN_TILE, D_MODEL), lambda i: (jnp.maximum(i - n_p, 0), 0)),
        ] + _ffn_weight_specs(layer),
        out_specs=_row_tile(),
        out_shape=jax.ShapeDtypeStruct((xp.shape[0] + xs.shape[0], D_MODEL), F32),
        scratch_shapes=[pltpu.VMEM((TOKEN_TILE, FFN_DIM), BF16)],
        compiler_params=_params("arbitrary"),
        name="ffn_first",
    )(xp, xs, g, w_in, w_out)


def _conv_gate(x_ref, g_ref, win_ref):
    x = x_ref[...]
    h = _rms(x, g_ref[...], NORM_EPS).astype(BF16)
    bcx = jnp.dot(h, win_ref[...], preferred_element_type=F32)
    b_gate = bcx[:, :D_MODEL]
    u = bcx[:, D_MODEL:2 * D_MODEL] * bcx[:, 2 * D_MODEL:]
    return x, b_gate, u


def _conv_finish(x, b_gate, u, u1, u2, wc_ref, wout_ref, o_ref):
    v = wc_ref[0:1, :] * u2 + wc_ref[1:2, :] * u1 + wc_ref[2:3, :] * u
    y = jnp.dot((b_gate * v).astype(BF16), wout_ref[...], preferred_element_type=F32)
    o_ref[...] = x + y


def _conv_prompt_kernel(tiles_per_seq, x_ref, g_ref, win_ref, wc_ref, wout_ref,
                        o_ref, tail_ref, ubuf):
    @pl.when(pl.program_id(0) % tiles_per_seq == 0)
    def _():
        ubuf[0:V7X_SUBLANES, :] = jnp.zeros((V7X_SUBLANES, D_MODEL), F32)

    x, b_gate, u = _conv_gate(x_ref, g_ref, win_ref)
    ubuf[V7X_SUBLANES:, :] = u
    u1 = ubuf[V7X_SUBLANES - 1:V7X_SUBLANES - 1 + TOKEN_TILE, :]
    u2 = ubuf[V7X_SUBLANES - 2:V7X_SUBLANES - 2 + TOKEN_TILE, :]
    _conv_finish(x, b_gate, u, u1, u2, wc_ref, wout_ref, o_ref)
    tail = u[TOKEN_TILE - V7X_SUBLANES:, :]
    tail_ref[...] = tail
    ubuf[0:V7X_SUBLANES, :] = tail


def _conv_weight_specs(layer):
    return [
        _resident((None, 1, D_MODEL), (N_MIXERS * layer, 0, 0)),
        _resident((None, D_MODEL, 3 * D_MODEL), (layer, 0, 0)),
        _resident((None, CONV_W, D_MODEL), (layer, 0, 0)),
        _resident((None, D_MODEL, D_MODEL), (layer, 0, 0)),
    ]


def _conv_prompt(x, g, w_in, w_conv, w_out, layer, n_tiles, seq_len):
    return pl.pallas_call(
        functools.partial(_conv_prompt_kernel, seq_len // TOKEN_TILE),
        grid=(n_tiles,),
        in_specs=[_row_tile()] + _conv_weight_specs(layer),
        out_specs=[_row_tile(), pl.BlockSpec((None, V7X_SUBLANES, D_MODEL), lambda i: (i, 0, 0))],
        out_shape=[jax.ShapeDtypeStruct(x.shape, F32),
                   jax.ShapeDtypeStruct((n_tiles, V7X_SUBLANES, D_MODEL), F32)],
        scratch_shapes=[pltpu.VMEM((V7X_SUBLANES + TOKEN_TILE, D_MODEL), F32)],
        input_output_aliases={0: 0},
        compiler_params=_params("arbitrary"),
        name="conv_prompt",
    )(x, g, w_in, w_conv, w_out)


def _conv_sample_kernel(n_seq, x_ref, g_ref, win_ref, wc_ref, wout_ref, p0_ref, p1_ref,
                        o_ref, st_ref):
    x, b_gate, u = _conv_gate(x_ref, g_ref, win_ref)
    n = x.shape[0]
    u1 = jnp.concatenate([p1_ref[...], u[:n - n_seq, :]], axis=0)
    u2 = jnp.concatenate([p0_ref[...], p1_ref[...], u[:n - 2 * n_seq, :]], axis=0)
    _conv_finish(x, b_gate, u, u1, u2, wc_ref, wout_ref, o_ref)
    st_ref[...] = u[n - 2 * n_seq:, :]


def _conv_sample(x, g, w_in, w_conv, w_out, prev0, prev1, layer, tile_offset):
    n_seq = prev0.shape[0]
    prev = pl.BlockSpec((n_seq, D_MODEL), lambda i: (0, 0))
    return pl.pallas_call(
        functools.partial(_conv_sample_kernel, n_seq),
        grid=(1,),
        in_specs=[_row_tile(tile_offset)] + _conv_weight_specs(layer) + [prev, prev],
        out_specs=[_row_tile(tile_offset), pl.BlockSpec((2 * n_seq, D_MODEL), lambda i: (0, 0))],
        out_shape=[jax.ShapeDtypeStruct(x.shape, F32),
                   jax.ShapeDtypeStruct((2 * n_seq, D_MODEL), F32)],
        input_output_aliases={0: 0},
        compiler_params=_params("arbitrary"),
        name="conv_sample",
    )(x, g, w_in, w_conv, w_out, prev0, prev1)


def _rope_angles(pos):
    half = ROT_DIM // 2
    inv = jnp.power(ROPE_THETA, -jnp.arange(half, dtype=F32) * (2.0 / ROT_DIM))
    ang = pos.astype(F32)[:, None] * inv[None, :]
    return jnp.cos(ang), jnp.sin(ang)


def _rope_lane_tables(pos):
    cos, sin = _rope_angles(pos)
    n, half = cos.shape
    ones = jnp.ones((n, HEAD_DIM - ROT_DIM), F32)
    zeros = jnp.zeros((n, HEAD_DIM - ROT_DIM), F32)
    zhalf = jnp.zeros((n, half), F32)
    c = jnp.concatenate([cos, cos, ones], axis=1)
    s1 = jnp.concatenate([-sin, zhalf, zeros], axis=1)
    s2 = jnp.concatenate([zhalf, sin, zeros], axis=1)
    rep = V7X_LANES // HEAD_DIM
    return tuple(jnp.tile(t, (1, rep)) for t in (c, s1, s2))


def _rope_lanes(z, c, s1, s2):
    half = ROT_DIM // 2
    return (z * c + pltpu.roll(z, V7X_LANES - half, 1) * s1 + pltpu.roll(z, half, 1) * s2)


def _qkv_sample_kernel(x_ref, g_ref, w_ref, c_ref, s1_ref, s2_ref,
                       qb_ref, kf_ref, vf_ref, kb_ref, vb_ref):
    h = _rms(x_ref[...], g_ref[...], NORM_EPS).astype(BF16)
    qkv = jnp.dot(h, w_ref[...], preferred_element_type=F32)
    c, s1, s2 = c_ref[...], s1_ref[...], s2_ref[...]
    for blk in range(HV // V7X_LANES):
        cols = slice(blk * V7X_LANES, (blk + 1) * V7X_LANES)
        q = _rope_lanes(qkv[:, cols], c, s1, s2)
        qb_ref[:, cols] = (q * (HEAD_DIM ** -0.5)).astype(BF16)
        k = _rope_lanes(qkv[:, HV + blk * V7X_LANES:HV + (blk + 1) * V7X_LANES], c, s1, s2)
        kf_ref[:, cols] = k
        kb_ref[:, cols] = k.astype(BF16)
    v = qkv[:, 2 * HV:]
    vf_ref[...] = v
    vb_ref[...] = v.astype(BF16)


def _qkv_sample(x, g, w_qkv, tables, layer, tile_offset):
    tile = pl.BlockSpec((TOKEN_TILE, HV), lambda i: (i, 0))
    tab = pl.BlockSpec((TOKEN_TILE, V7X_LANES), lambda i: (i, 0))
    shp = lambda dt: jax.ShapeDtypeStruct((TOKEN_TILE, HV), dt)
    return pl.pallas_call(
        _qkv_sample_kernel,
        grid=(1,),
        in_specs=[
            _row_tile(tile_offset),
            _resident((None, 1, D_MODEL), (N_MIXERS * layer + 1, 0, 0)),
            _resident((None, D_MODEL, 3 * HV), (layer, 0, 0)),
            tab, tab, tab,
        ],
        out_specs=[tile] * 5,
        out_shape=[shp(BF16), shp(F32), shp(F32), shp(BF16), shp(BF16)],
        compiler_params=_params("arbitrary"),
        name="qkv_sample",
    )(x, g, w_qkv, *tables)


def _qkv_prompt_kernel(x_ref, g_ref, wq_ref, wkt_ref, wv_ref, c_ref, s1_ref, s2_ref,
                       ct_ref, st_ref, *rest):
    qb_ref, ktb_ref, vb_ref, kf_ref, vf_ref = rest[-5:]
    h = _rms(x_ref[...], g_ref[...], NORM_EPS).astype(BF16)

    q = jnp.dot(h, wq_ref[...], preferred_element_type=F32)
    c, s1, s2 = c_ref[...], s1_ref[...], s2_ref[...]
    for blk in range(HV // V7X_LANES):
        cols = slice(blk * V7X_LANES, (blk + 1) * V7X_LANES)
        qb_ref[:, cols] = (_rope_lanes(q[:, cols], c, s1, s2) * (HEAD_DIM ** -0.5)).astype(BF16)

    kt = lax.dot_general(wkt_ref[...], h, _NT, preferred_element_type=F32)
    ct, st = ct_ref[...], st_ref[...]
    half = ROT_DIM // 2
    for blk in range(HV // HEAD_DIM):
        r0 = blk * HEAD_DIM
        x1 = kt[r0:r0 + half, :]
        x2 = kt[r0 + half:r0 + ROT_DIM, :]
        rot = jnp.concatenate([x1 * ct - x2 * st, x2 * ct + x1 * st], axis=0)
        rest_rows = kt[r0 + ROT_DIM:r0 + HEAD_DIM, :]
        kf_ref[r0:r0 + ROT_DIM, :] = rot
        kf_ref[r0 + ROT_DIM:r0 + HEAD_DIM, :] = rest_rows
        ktb_ref[r0:r0 + ROT_DIM, :] = rot.astype(BF16)
        ktb_ref[r0 + ROT_DIM:r0 + HEAD_DIM, :] = rest_rows.astype(BF16)

    v = jnp.dot(h, wv_ref[...], preferred_element_type=F32)
    vb_ref[...] = v.astype(BF16)
    for hh in range(N_HEADS):
        vf_ref[pl.ds(hh, TOKEN_TILE, stride=N_HEADS), :] = v[:, hh * VAL_DIM:(hh + 1) * VAL_DIM]


def _qkv_prompt(x, g, wq, wkt, wv, lane_tables, sub_tables, layer, batch, seq_len, n_att,
                kf_prev=None, vf_prev=None):
    tps = seq_len // TOKEN_TILE
    n_tok = batch * seq_len
    tile = pl.BlockSpec((TOKEN_TILE, HV), lambda i: (i, 0))
    tab = pl.BlockSpec((TOKEN_TILE, V7X_LANES), lambda i: (i % tps, 0))
    tab_t = pl.BlockSpec((ROT_DIM // 2, TOKEN_TILE), lambda i: (0, i % tps))
    w_spec = _resident((None, HV, D_MODEL), (layer, 0, 0))
    in_specs = [
        _row_tile(),
        _resident((None, 1, D_MODEL), (N_MIXERS * layer + 1, 0, 0)),
        w_spec, w_spec, w_spec,
        tab, tab, tab, tab_t, tab_t,
    ]
    args = [x, g, wq, wkt, wv, *lane_tables, *sub_tables]
    aliases = {}
    if kf_prev is not None:
        aliases = {len(args): 3, len(args) + 1: 4}
        in_specs += [_ANY, _ANY]
        args += [kf_prev, vf_prev]
    out_specs = [
        tile,
        pl.BlockSpec((None, HV, TOKEN_TILE), lambda i: (i // tps, 0, i % tps)),
        tile,
        pl.BlockSpec((None, None, HV, TOKEN_TILE), lambda i: (i // tps, layer, 0, i % tps)),
        pl.BlockSpec((None, None, TOKEN_TILE * N_HEADS, VAL_DIM),
                     lambda i: (i // tps, layer, i % tps, 0)),
    ]
    out_shape = [
        jax.ShapeDtypeStruct((n_tok, HV), BF16),
        jax.ShapeDtypeStruct((batch, HV, seq_len), BF16),
        jax.ShapeDtypeStruct((n_tok, HV), BF16),
        jax.ShapeDtypeStruct((batch, n_att, HV, seq_len), F32),
        jax.ShapeDtypeStruct((batch, n_att, seq_len * N_HEADS, VAL_DIM), F32),
    ]
    return pl.pallas_call(
        _qkv_prompt_kernel,
        grid=(n_tok // TOKEN_TILE,),
        in_specs=in_specs,
        out_specs=out_specs,
        out_shape=out_shape,
        input_output_aliases=aliases,
        compiler_params=_params("arbitrary"),
        name="qkv_prompt",
    )(*args)


def _diff_lambda(lam_ref, lam_init):
    lv = lam_ref[...]
    s1 = jnp.sum(lv[0:1, :] * lv[1:2, :], axis=-1, keepdims=True)
    s2 = jnp.sum(lv[2:3, :] * lv[3:4, :], axis=-1, keepdims=True)
    return jnp.exp(s1) - jnp.exp(s2) + lam_init


def _map_mask(q, c):
    lane = lax.broadcasted_iota(jnp.int32, q.shape, 1)
    keep = (lane < HEAD_DIM) if c == 0 else (lane >= HEAD_DIM)
    return jnp.where(keep, q, jnp.zeros_like(q))


def _attn_prompt_kernel(lam_init, seq_len, lam_ref, q_ref, kt_ref, v_ref, o_ref):
    tq = ATTN_Q_TILE
    lam = _diff_lambda(lam_ref, lam_init)
    row = lax.broadcasted_iota(jnp.int32, (2 * tq, tq), 0)
    col = lax.broadcasted_iota(jnp.int32, (2 * tq, tq), 1)
    causal = col <= jnp.where(row >= tq, row - tq, row)
    for qi in range(seq_len // tq):
        lo, hi = qi * tq, (qi + 1) * tq
        q = q_ref[lo:hi, :]
        q2 = jnp.concatenate([_map_mask(q, 0), _map_mask(q, 1)], axis=0)
        s = jnp.dot(q2, kt_ref[:, :hi], preferred_element_type=F32)
        s_diag = jnp.where(causal, s[:, lo:], NEG_INF)
        m = jnp.max(s_diag, axis=-1, keepdims=True)
        if qi:
            m = jnp.maximum(m, jnp.max(s[:, :lo], axis=-1, keepdims=True))
        e_diag = jnp.exp(s_diag - m)
        l = jnp.sum(e_diag, axis=-1, keepdims=True)
        e = e_diag.astype(BF16)
        if qi:
            e_off = jnp.exp(s[:, :lo] - m)
            l = l + jnp.sum(e_off, axis=-1, keepdims=True)
            e = jnp.concatenate([e_off.astype(BF16), e], axis=1)
        o2 = jnp.dot(e, v_ref[:hi, :], preferred_element_type=F32) * (1.0 / l)
        o_ref[lo:hi, :] = o2[:tq, :] - lam * o2[tq:, :]


def _attn_prompt(lams, qb, ktb, vb, batch, seq_len, n_rows, lam_init):
    qv_spec = pl.BlockSpec((seq_len, VAL_DIM), lambda b, h: (b, h))
    return pl.pallas_call(
        functools.partial(_attn_prompt_kernel, lam_init, seq_len),
        grid=(batch, N_HEADS),
        in_specs=[
            pl.BlockSpec(lams.shape, lambda b, h: (0, 0)),
            qv_spec,
            pl.BlockSpec((None, VAL_DIM, seq_len), lambda b, h: (b, h, 0)),
            qv_spec,
        ],
        out_specs=qv_spec,
        out_shape=jax.ShapeDtypeStruct((n_rows, HV), F32),
        compiler_params=_params("arbitrary", "arbitrary"),
        name="attn_prompt",
    )(lams, qb, ktb, vb)


def _attn_sample_kernel(layer, n_pages, page_size, n_new, lam_init,
                        pt_ref, lam_ref, q_ref, k_ref, v_ref, ck_ref, cv_ref, o_ref,
                        kbuf, vbuf, sem_k, sem_v):
    b = pl.program_id(0)
    nb = pl.num_programs(0)
    rows_per_page = page_size * N_HEADS

    def copies(bb, slot):
        out = []
        for p in range(n_pages):
            page = pt_ref[bb, p]
            out.append(pltpu.make_async_copy(
                ck_ref.at[page, layer],
                kbuf.at[slot, :, pl.ds(p * page_size, page_size)], sem_k.at[slot]))
            out.append(pltpu.make_async_copy(
                cv_ref.at[page, layer],
                vbuf.at[slot, pl.ds(p * rows_per_page, rows_per_page), :], sem_v.at[slot]))
        return out

    slot = b % 2

    @pl.when(b == 0)
    def _():
        for cp in copies(0, 0):
            cp.start()

    @pl.when(b + 1 < nb)
    def _():
        for cp in copies(b + 1, 1 - slot):
            cp.start()

    for cp in copies(b, slot):
        cp.wait()

    lam = _diff_lambda(lam_ref, lam_init)
    q = q_ref[...]
    k_new = k_ref[...]
    v_new = v_ref[...]
    m8 = q.shape[0]
    past = n_pages * page_size
    row = lax.broadcasted_iota(jnp.int32, (m8, m8), 0)
    col = lax.broadcasted_iota(jnp.int32, (m8, m8), 1)
    new_ok = (col <= row) & (col < n_new)
    for h in range(N_HEADS):
        probs = []
        for c in range(2):
            lo = (2 * h + c) * HEAD_DIM
            qhc = q[:, lo:lo + HEAD_DIM]
            kt = kbuf[slot, lo:lo + HEAD_DIM, :].astype(BF16)
            s_past = jnp.dot(qhc, kt, preferred_element_type=F32)
            s_new = lax.dot_general(qhc, k_new[:, lo:lo + HEAD_DIM], _NT,
                                    preferred_element_type=F32)
            s_new = jnp.where(new_ok, s_new, NEG_INF)
            m = jnp.maximum(jnp.max(s_past, axis=-1, keepdims=True),
                            jnp.max(s_new, axis=-1, keepdims=True))
            p_past = jnp.exp(s_past - m)
            p_new = jnp.exp(s_new - m)
            inv = 1.0 / (jnp.sum(p_past, axis=-1, keepdims=True)
                         + jnp.sum(p_new, axis=-1, keepdims=True))
            probs.append((p_past * inv, p_new * inv))
        pd_past = (probs[0][0] - lam * probs[1][0]).astype(BF16)
        pd_new = (probs[0][1] - lam * probs[1][1]).astype(BF16)
        vh = vbuf[slot, pl.ds(h, past, stride=N_HEADS), :].astype(BF16)
        o = (jnp.dot(pd_past, vh, preferred_element_type=F32)
             + jnp.dot(pd_new, v_new[:, h * VAL_DIM:(h + 1) * VAL_DIM],
                       preferred_element_type=F32))
        o_ref[:, h * VAL_DIM:(h + 1) * VAL_DIM] = o


def _attn_sample(page_table, lams, q8, k8, v8, ck, cv, layer, lam_init, n_new):
    n_seq, n_pages = page_table.shape
    page_size = ck.shape[-1]
    past = n_pages * page_size
    m8 = q8.shape[1]
    per_seq = pl.BlockSpec((None, m8, HV), lambda b, pt: (b, 0, 0))
    grid_spec = pltpu.PrefetchScalarGridSpec(
        num_scalar_prefetch=1,
        grid=(n_seq,),
        in_specs=[
            pl.BlockSpec(lams.shape, lambda b, pt: (0, 0)),
            per_seq, per_seq, per_seq, _ANY, _ANY,
        ],
        out_specs=per_seq,
        scratch_shapes=[
            pltpu.VMEM((2, 2 * N_HEADS * HEAD_DIM, past), F32),
            pltpu.VMEM((2, past * N_HEADS, VAL_DIM), F32),
            pltpu.SemaphoreType.DMA((2,)),
            pltpu.SemaphoreType.DMA((2,)),
        ],
    )
    return pl.pallas_call(
        functools.partial(_attn_sample_kernel, layer, n_pages, page_size, n_new, lam_init),
        grid_spec=grid_spec,
        out_shape=jax.ShapeDtypeStruct((n_seq, m8, HV), F32),
        compiler_params=_params("arbitrary"),
        name="attn_sample",
    )(page_table, lams, q8, k8, v8, ck, cv)


def _attn_out_kernel(lam_init, x_ref, o_ref, g_ref, wo_ref, out_ref, hb_ref):
    for h in range(N_HEADS):
        cols = slice(h * VAL_DIM, (h + 1) * VAL_DIM)
        y = _rms(o_ref[:, cols], g_ref[...], SUBLN_EPS) * (1.0 - lam_init)
        hb_ref[:, cols] = y.astype(BF16)
    out_ref[...] = x_ref[...] + jnp.dot(hb_ref[...], wo_ref[...], preferred_element_type=F32)


def _attn_out(x, o, g_subln, w_o, layer, lam_init):
    return pl.pallas_call(
        functools.partial(_attn_out_kernel, lam_init),
        grid=(x.shape[0] // TOKEN_TILE,),
        in_specs=[
            _row_tile(), _row_tile(),
            _resident((None, 1, VAL_DIM), (layer, 0, 0)),
            _resident((None, HV, D_MODEL), (layer, 0, 0)),
        ],
        out_specs=_row_tile(),
        out_shape=jax.ShapeDtypeStruct(x.shape, F32),
        scratch_shapes=[pltpu.VMEM((TOKEN_TILE, HV), BF16)],
        input_output_aliases={0: 0},
        compiler_params=_params("arbitrary"),
        name="attn_out",
    )(x, o, g_subln, w_o)


def _final_norm_kernel(x_ref, g_ref, o_ref):
    o_ref[...] = _rms(x_ref[...], g_ref[...], NORM_EPS)


def _final_norm(x, g, tile_offset, n_tiles):
    return pl.pallas_call(
        _final_norm_kernel,
        grid=(n_tiles,),
        in_specs=[_row_tile(tile_offset), pl.BlockSpec((1, D_MODEL), lambda i: (0, 0))],
        out_specs=_row_tile(),
        out_shape=jax.ShapeDtypeStruct((n_tiles * TOKEN_TILE, D_MODEL), F32),
        compiler_params=_params("arbitrary"),
        name="final_norm",
    )(x, g)


def _pad_rows(x, rows):
    return jnp.concatenate(
        [x, jnp.zeros((x.shape[0], rows - x.shape[1], x.shape[2]), x.dtype)], axis=1)


def kernel(x_prompt, x_sample, cache_k, cache_v, state_conv, page_table, g_ffn1, w_ffn1_in, w_ffn1_out, g_mix, w_conv_in, w_conv, w_conv_out, w_qkv, lambda_q1, lambda_k1, lambda_q2, lambda_k2, g_subln, w_o, g_ffn2, w_ffn2_in, w_ffn2_out, g_final):
    batch, seq_len, _ = x_prompt.shape
    n_seq, n_new, _ = x_sample.shape
    depth = g_ffn1.shape[0]
    n_pool, n_att, page_size = cache_k.shape[:3]
    past_len = page_table.shape[1] * page_size
    n_p = batch * seq_len
    n_s = n_new * n_seq
    p_tiles = n_p // TOKEN_TILE

    w1i, w1o = w_ffn1_in.astype(BF16), w_ffn1_out.astype(BF16)
    w2i, w2o = w_ffn2_in.astype(BF16), w_ffn2_out.astype(BF16)
    wci, wco = w_conv_in.astype(BF16), w_conv_out.astype(BF16)
    wqkv, wo = w_qkv.astype(BF16), w_o.astype(BF16)
    wq = wqkv[:, :, :HV]
    wkt = jnp.transpose(wqkv[:, :, HV:2 * HV], (0, 2, 1))
    wv = wqkv[:, :, 2 * HV:]
    g1 = g_ffn1.reshape(depth, 1, D_MODEL)
    g2 = g_ffn2.reshape(depth, 1, D_MODEL)
    gm = g_mix.reshape(depth, 1, D_MODEL)
    gs = g_subln.reshape(n_att, 1, VAL_DIM)
    lams = jnp.stack([lambda_q1, lambda_k1, lambda_q2, lambda_k2], axis=1)

    ck = jnp.transpose(cache_k, (0, 1, 3, 4, 5, 2)).reshape(
        n_pool, n_att, 2 * N_HEADS * HEAD_DIM, page_size)
    cv = cache_v.reshape(n_pool, n_att, page_size * N_HEADS, VAL_DIM)

    pos_p = jnp.arange(seq_len)
    tab_p = _rope_lane_tables(pos_p)
    tab_pt = tuple(jnp.transpose(t) for t in _rope_angles(pos_p))
    tab_s = _rope_lane_tables(past_len + jnp.arange(n_s) // n_seq)

    xs0 = jnp.transpose(x_sample, (1, 0, 2)).reshape(n_s, D_MODEL)
    x = None
    kf_all = vf_all = None
    cp, ks, vs, cs = [], [], [], []
    for i in range(depth):
        if i == 0:
            x = _ffn_first(x_prompt.reshape(n_p, D_MODEL), xs0, g1, w1i, w1o, i)
        else:
            x = _ffn(x, g1, w1i, w1o, i)
        j = i // N_MIXERS
        if i % N_MIXERS == 0:
            x, tails = _conv_prompt(x, gm, wci, w_conv, wco, j, p_tiles, seq_len)
            tps = seq_len // TOKEN_TILE
            cp.append(tails.reshape(batch, tps, V7X_SUBLANES, D_MODEL)[:, -1, -(CONV_W - 1):, :])
            x, st = _conv_sample(x, gm, wci, w_conv, wco,
                                 state_conv[:, j, 0, :], state_conv[:, j, 1, :], j, p_tiles)
            cs.append(jnp.transpose(st.reshape(CONV_W - 1, n_seq, D_MODEL), (1, 0, 2)))
        else:
            lam_init = 0.8 - 0.6 * math.exp(-0.3 * i)
            qb, ktb, vb, kf_all, vf_all = _qkv_prompt(
                x, gm, wq, wkt, wv, tab_p, tab_pt, j, batch, seq_len, n_att, kf_all, vf_all)
            o = _attn_prompt(lams[j], qb, ktb, vb, batch, seq_len, n_p + n_s, lam_init)

            qb, kf, vf, kb, vb = _qkv_sample(x, gm, wqkv, tab_s, j, p_tiles)
            to_seq = lambda a: _pad_rows(
                jnp.transpose(a.reshape(n_new, n_seq, HV), (1, 0, 2)), V7X_SUBLANES)
            o8 = _attn_sample(page_table, lams[j], to_seq(qb), to_seq(kb), to_seq(vb),
                              ck, cv, j, lam_init, n_new)
            o_s = jnp.transpose(o8[:, :n_new, :], (1, 0, 2)).reshape(n_s, HV)
            o = lax.dynamic_update_slice(o, o_s, (n_p, 0))
            x = _attn_out(x, o, gs, wo, j, lam_init)
            ks.append(jnp.transpose(kf.reshape(n_new, n_seq, N_HEADS, 2, HEAD_DIM), (1, 0, 2, 3, 4)))
            vs.append(jnp.transpose(vf.reshape(n_new, n_seq, N_HEADS, VAL_DIM), (1, 0, 2, 3)))
        x = _ffn(x, g2, w2i, w2o, i)

    gf = g_final.reshape(1, D_MODEL)
    y_prompt = _final_norm(x, gf, 0, p_tiles).reshape(batch, seq_len, D_MODEL)
    y_sample = jnp.transpose(
        _final_norm(x, gf, p_tiles, n_s // TOKEN_TILE).reshape(n_new, n_seq, D_MODEL), (1, 0, 2))
    k_rows_prompt = jnp.transpose(
        kf_all.reshape(batch, n_att, N_HEADS, 2, HEAD_DIM, seq_len), (0, 1, 5, 2, 3, 4))
    v_rows_prompt = vf_all.reshape(batch, n_att, seq_len, N_HEADS, VAL_DIM)
    return (y_prompt, y_sample, k_rows_prompt, v_rows_prompt, jnp.stack(cp, axis=1),
            jnp.stack(ks, axis=1), jnp.stack(vs, axis=1), jnp.stack(cs, axis=1))
```
